```python
import math
import jax, jax.numpy as jnp
from jax import lax
import numpy as np

D_MODEL = 1024
BATCH = 8
SEQ = 4096
DEPTH = 4

BLOCK = 128
EPS = 1e-6
NEG = -1e30

MLA_HEADS = 8
MLA_Q_RANK = 256
MLA_KV_RANK = 128
MLA_NOPE = 64
MLA_ROPE = 32
MLA_V = 64
MLA_WIDTH = MLA_HEADS * MLA_V
ROPE_THETA = 10000.0

SWA_HEADS = 8
SWA_KV_HEADS = 2
SWA_GROUP = SWA_HEADS // SWA_KV_HEADS
SWA_HEAD_DIM = 64
SWA_WINDOW = 128
SWA_WIDTH = SWA_HEADS * SWA_HEAD_DIM
SWA_KV_WIDTH = SWA_KV_HEADS * SWA_HEAD_DIM

DIFF_HEADS = 4
DIFF_HEAD_DIM = 64
DIFF_WIDTH = DIFF_HEADS * 2 * DIFF_HEAD_DIM

REL_BUCKETS = 32
REL_MAX_DIST = 128
REL_HEADS = SWA_HEADS + DIFF_HEADS

N_BRANCH = 3

IN_SPLITS = (MLA_Q_RANK, MLA_KV_RANK, MLA_ROPE, MLA_WIDTH,
             SWA_WIDTH, SWA_KV_WIDTH, SWA_KV_WIDTH, SWA_WIDTH,
             DIFF_WIDTH, DIFF_WIDTH, DIFF_WIDTH, DIFF_WIDTH,
             N_BRANCH * D_MODEL)
IN_COLS = sum(IN_SPLITS)

kernel_name = "hybrid_mla_swa_diff_encoder"


def rms_norm(x, g):
    xf = x.astype(jnp.float32)
    y = xf * lax.rsqrt(jnp.mean(xf * xf, axis=-1, keepdims=True) + EPS)
    return (y * g.astype(jnp.float32)).astype(x.dtype)


def split_cols(t, sizes):
    out, o = [], 0
    for n in sizes:
        out.append(t[..., o:o + n])
        o += n
    return out


def rope(x, cos, sin):
    half = x.shape[-1] // 2
    x1, x2 = x[..., :half], x[..., half:]
    cos = cos.astype(x.dtype)
    sin = sin.astype(x.dtype)
    return jnp.concatenate([x1 * cos - x2 * sin, x2 * cos + x1 * sin], axis=-1)


def t5_bucket(rel):
    nb = REL_BUCKETS // 2
    max_exact = nb // 2
    base = jnp.where(rel > 0, nb, 0)
    n = jnp.abs(rel)
    nf = jnp.maximum(n, 1).astype(jnp.float32)
    large = max_exact + (jnp.log(nf / max_exact) / math.log(REL_MAX_DIST / max_exact)
                         * (nb - max_exact)).astype(jnp.int32)
    large = jnp.minimum(large, nb - 1)
    return base + jnp.where(n < max_exact, n, large)


def mla_branch(c_q, c_kv, k_r, g_q, w_uq, g_kv, w_ukv, cos, sin):
    B, S, _ = c_q.shape
    q = (rms_norm(c_q, g_q) @ w_uq).reshape(B, S, MLA_HEADS, MLA_NOPE + MLA_ROPE)
    q = jnp.concatenate([q[..., :MLA_NOPE],
                         rope(q[..., MLA_NOPE:], cos[None, :, None, :], sin[None, :, None, :])], axis=-1)
    kv = (rms_norm(c_kv, g_kv) @ w_ukv).reshape(B, S, MLA_HEADS, MLA_NOPE + MLA_V)
    k_pe = rope(k_r, cos[None], sin[None])
    k = jnp.concatenate([kv[..., :MLA_NOPE],
                         jnp.broadcast_to(k_pe[:, :, None, :], (B, S, MLA_HEADS, MLA_ROPE))], axis=-1)
    v = kv[..., MLA_NOPE:]
    scale = (MLA_NOPE + MLA_ROPE) ** -0.5

    def attend(qb):
        s = jnp.einsum('bqhd,bkhd->bhqk', qb, k).astype(jnp.float32) * scale
        p = jax.nn.softmax(s, axis=-1).astype(v.dtype)
        return jnp.einsum('bhqk,bkhd->bqhd', p, v)

    nq = S // BLOCK
    qb = q.reshape(B, nq, BLOCK, MLA_HEADS, MLA_NOPE + MLA_ROPE).swapaxes(0, 1)
    o = lax.map(attend, qb)
    return o.swapaxes(0, 1).reshape(B, S, MLA_WIDTH)


def swa_branch(q, k, v, sink, bias_band):
    B, S, _ = q.shape
    nb = S // BLOCK
    q = q.reshape(B, nb, BLOCK, SWA_KV_HEADS, SWA_GROUP, SWA_HEAD_DIM)

    def band(t):
        t = t.reshape(B, S, SWA_KV_HEADS, SWA_HEAD_DIM)
        tp = jnp.pad(t, ((0, 0), (BLOCK, BLOCK), (0, 0), (0, 0)))
        tp = tp.reshape(B, nb + 2, BLOCK, SWA_KV_HEADS, SWA_HEAD_DIM)
        return jnp.concatenate([tp[:, :-2], tp[:, 1:-1], tp[:, 2:]], axis=2)

    kb, vb = band(k), band(v)
    s = jnp.einsum('bnqgrd,bnkgd->bngrqk', q, kb).astype(jnp.float32) * SWA_HEAD_DIM ** -0.5
    s = s + bias_band.reshape(SWA_KV_HEADS, SWA_GROUP, BLOCK, 3 * BLOCK)[None, None]
    t = jnp.arange(BLOCK)
    j = jnp.arange(3 * BLOCK)
    blk = jnp.arange(nb)
    rel = j[None, :] - BLOCK - t[:, None]
    kpos = blk[:, None] * BLOCK - BLOCK + j[None, :]
    valid = (jnp.abs(rel) <= SWA_WINDOW)[None] & ((kpos >= 0) & (kpos < S))[:, None, :]
    s = jnp.where(valid[None, :, None, None], s, NEG)
    sink_l = sink.astype(jnp.float32).reshape(SWA_KV_HEADS, SWA_GROUP)[None, None, :, :, None, None]
    m = jnp.maximum(jnp.max(s, axis=-1, keepdims=True), sink_l)
    p = jnp.exp(s - m)
    p = p / (jnp.sum(p, axis=-1, keepdims=True) + jnp.exp(sink_l - m))
    o = jnp.einsum('bngrqk,bnkgd->bnqgrd', p.astype(v.dtype), vb)
    return o.reshape(B, S, SWA_WIDTH)


def diff_branch(q, k, v, lam_q1, lam_k1, lam_q2, lam_k2, g_sub, rel_bias, lam_init):
    B, S, _ = q.shape
    H, d = DIFF_HEADS, DIFF_HEAD_DIM
    q = q.reshape(B, S, H, 2, d)
    k = k.reshape(B, S, H, 2, d)
    v = v.reshape(B, S, H, 2 * d)
    f32 = jnp.float32
    lam = (jnp.exp(jnp.sum(lam_q1.astype(f32) * lam_k1.astype(f32)))
           - jnp.exp(jnp.sum(lam_q2.astype(f32) * lam_k2.astype(f32))) + lam_init)
    table = rel_bias[:, SWA_HEADS:].astype(f32)
    kpos = jnp.arange(S)

    def attend(args):
        i, qb = args
        qpos = i * BLOCK + jnp.arange(BLOCK)
        bias = table[t5_bucket(kpos[None, :] - qpos[:, None])].transpose(2, 0, 1)
        s = jnp.einsum('bqhcd,bkhcd->bchqk', qb, k).astype(f32) * d ** -0.5 + bias[None, None]
        p = jax.nn.softmax(s, axis=-1)
        a = p[:, 0] - lam * p[:, 1]
        return jnp.einsum('bhqk,bkhe->bqhe', a.astype(v.dtype), v)

    nq = S // BLOCK
    qb = q.reshape(B, nq, BLOCK, H, 2, d).swapaxes(0, 1)
    o = lax.map(attend, (jnp.arange(nq), qb))
    o = o.swapaxes(0, 1).reshape(B, S, H, 2 * d)
    o = rms_norm(o, g_sub) * (1.0 - lam_init)
    return o.reshape(B, S, DIFF_WIDTH)


def setup_inputs(seed: int = 0) -> dict:
    key = jax.random.key(seed)
    ks = jax.random.split(key, 24)
    D, L = D_MODEL, DEPTH
    nrm = jax.random.normal
    f32 = jnp.float32
    return {
        "x": nrm(ks[0], (BATCH, SEQ, D), f32),
        "c": nrm(ks[1], (BATCH, D), f32),
        "w_ada": nrm(ks[2], (L, D, 3 * D), f32) * (0.5 * D ** -0.5),
        "b_ada": nrm(ks[3], (L, 3 * D), f32) * 0.01,
        "g_pre": 1.0 + 0.1 * nrm(ks[4], (L, D), f32),
        "g_post": 1.0 + 0.1 * nrm(ks[5], (L, D), f32),
        "w_in": nrm(ks[6], (L, D, IN_COLS), f32) * D ** -0.5,
        "g_q": 1.0 + 0.1 * nrm(ks[7], (L, MLA_Q_RANK), f32),
        "w_uq": nrm(ks[8], (L, MLA_Q_RANK, MLA_HEADS * (MLA_NOPE + MLA_ROPE)), f32) * MLA_Q_RANK ** -0.5,
        "g_kv": 1.0 + 0.1 * nrm(ks[9], (L, MLA_KV_RANK), f32),
        "w_ukv": nrm(ks[10], (L, MLA_KV_RANK, MLA_HEADS * (MLA_NOPE + MLA_V)), f32) * MLA_KV_RANK ** -0.5,
        "sink": nrm(ks[11], (L, SWA_HEADS), f32) * 0.5,
        "lam_q1": nrm(ks[12], (L, DIFF_HEAD_DIM), f32) * 0.1,
        "lam_k1": nrm(ks[13], (L, DIFF_HEAD_DIM), f32) * 0.1,
        "lam_q2": nrm(ks[14], (L, DIFF_HEAD_DIM), f32) * 0.1,
        "lam_k2": nrm(ks[15], (L, DIFF_HEAD_DIM), f32) * 0.1,
        "g_sub": 1.0 + 0.1 * nrm(ks[16], (L, 2 * DIFF_HEAD_DIM), f32),
        "w_o_mla": nrm(ks[17], (L, MLA_WIDTH, D), f32) * MLA_WIDTH ** -0.5,
        "w_o_swa": nrm(ks[18], (L, SWA_WIDTH, D), f32) * SWA_WIDTH ** -0.5,
        "w_o_diff": nrm(ks[19], (L, DIFF_WIDTH, D), f32) * DIFF_WIDTH ** -0.5,
        "w_out": nrm(ks[20], (L, D, D), f32) * D ** -0.5,
        "rel_bias": nrm(ks[21], (REL_BUCKETS, REL_HEADS), f32) * 0.5,
    }


def reference(x, c, w_ada, b_ada, g_pre, g_post, w_in, g_q, w_uq, g_kv, w_ukv, sink,
              lam_q1, lam_k1, lam_q2, lam_k2, g_sub, w_o_mla, w_o_swa, w_o_diff, w_out, rel_bias):
    B, S, D = x.shape
    pos = jnp.arange(S, dtype=jnp.float32)
    inv = ROPE_THETA ** (-jnp.arange(0, MLA_ROPE, 2, dtype=jnp.float32) / MLA_ROPE)
    ang = pos[:, None] * inv[None, :]
    cos, sin = jnp.cos(ang), jnp.sin(ang)
    rel_band = jnp.arange(3 * BLOCK)[None, :] - BLOCK - jnp.arange(BLOCK)[:, None]
    swa_bias = rel_bias[t5_bucket(rel_band)][..., :SWA_HEADS].astype(jnp.float32).transpose(2, 0, 1)
    c_act = jax.nn.silu(c)

    for l in range(DEPTH):
        mod = c_act @ w_ada[l] + b_ada[l]
        shift, scale, gate = [m[:, None, :] for m in split_cols(mod, (D, D, D))]
        h = rms_norm(x, g_pre[l]) * (1.0 + scale) + shift
        (cq, ckv, kr, za, qb_, kb_, vb_, zb, qc, kc, vc, zc, gcols) = split_cols(h @ w_in[l], IN_SPLITS)

        o_a = (mla_branch(cq, ckv, kr, g_q[l], w_uq[l], g_kv[l], w_ukv[l], cos, sin)
               * jax.nn.silu(za)) @ w_o_mla[l]
        o_b = (swa_branch(qb_, kb_, vb_, sink[l], swa_bias) * jax.nn.silu(zb)) @ w_o_swa[l]
        lam_init = 0.8 - 0.6 * math.exp(-0.3 * l)
        o_c = (diff_branch(qc, kc, vc, lam_q1[l], lam_k1[l], lam_q2[l], lam_k2[l], g_sub[l],
                           rel_bias, lam_init) * jax.nn.silu(zc)) @ w_o_diff[l]

        g = jax.nn.sigmoid(gcols).reshape(B, S, N_BRANCH, D)
        y = (g[:, :, 0] * o_a + g[:, :, 1] * o_b + g[:, :, 2] * o_c) @ w_out[l]
        x = x + gate * rms_norm(y, g_post[l])
    return x
```

```python
import functools
import math

import jax
import jax.numpy as jnp
from jax import lax
from jax.experimental import pallas as pl
from jax.experimental.pallas import tpu as pltpu

F32 = jnp.float32
BF16 = jnp.bfloat16

D_MODEL = 1024
EPS = 1e-6
NEG = -1e30

MLA_HEADS = 8
MLA_Q_RANK = 256
MLA_KV_RANK = 128
MLA_NOPE = 64
MLA_ROPE = 32
MLA_V = 64
ROPE_THETA = 10000.0

SWA_HEADS = 8
SWA_HEAD_DIM = 64
SWA_WINDOW = 128
SWA_BLOCK = 128

DIFF_HEADS = 4
DIFF_HEAD_DIM = 64

REL_BUCKETS = 32
REL_MAX_DIST = 128

LANES = 128
VMEM_LIMIT = 56 * 1024 * 1024

PROJ_TS = 512
ATT_TQ = 256
KEY_TILE = 256

_OFF = {}
_o = 0
for _name, _n in (("cq", 256), ("ckv", 128), ("kr", 32), ("za", 512),
                  ("qb", 512), ("kb", 128), ("vb", 128), ("zb", 512),
                  ("qc", 512), ("kc", 512), ("vc", 512), ("zc", 512), ("g", 3072)):
    _OFF[_name] = (_o, _o + _n)
    _o += _n

_SWA_HEAD_ORDER = (0, 4, 1, 5, 2, 6, 3, 7)


def _t5_bucket(rel):
    nb = REL_BUCKETS // 2
    max_exact = nb // 2
    base = jnp.where(rel > 0, nb, 0)
    n = jnp.abs(rel)
    nf = jnp.maximum(n, 1).astype(jnp.float32)
    large = max_exact + (jnp.log(nf / max_exact) / math.log(REL_MAX_DIST / max_exact)
                         * (nb - max_exact)).astype(jnp.int32)
    large = jnp.minimum(large, nb - 1)
    return base + jnp.where(n < max_exact, n, large)


def _rms(x, g):
    return x * lax.rsqrt(jnp.mean(x * x, axis=-1, keepdims=True) + EPS) * g


def _sigmoid(x):
    return 1.0 / (1.0 + jnp.exp(-x))


def _dot(a, b):
    return jnp.dot(a, b, preferred_element_type=F32)


def _dot_nt(a, b):
    return lax.dot_general(a, b, (((1,), (1,)), ((), ())), preferred_element_type=F32)


def _ada_kernel(c_ref, w_ref, b_ref, o_ref):
    c = c_ref[...]
    ca = (c * _sigmoid(c)).astype(BF16)
    o_ref[0] = _dot(ca, w_ref[0].astype(BF16)) + b_ref[0]


def _ada(c, w_ada, b_ada):
    L, D, D3 = w_ada.shape
    B = c.shape[0]
    return pl.pallas_call(
        _ada_kernel,
        out_shape=jax.ShapeDtypeStruct((L, B, D3), F32),
        grid=(L,),
        in_specs=[pl.BlockSpec((B, D), lambda l: (0, 0)),
                  pl.BlockSpec((1, D, D3), lambda l: (l, 0, 0)),
                  pl.BlockSpec((1, 1, D3), lambda l: (l, 0, 0))],
        out_specs=pl.BlockSpec((1, B, D3), lambda l: (l, 0, 0)),
        compiler_params=pltpu.CompilerParams(dimension_semantics=("arbitrary",),
                                             vmem_limit_bytes=VMEM_LIMIT),
        name="ada_mod",
    )(c, w_ada, b_ada.reshape(L, 1, D3))


def _proj_kernel(x_ref, shift_ref, scale_ref, gpre_ref, w_ref, gq_ref, gkv_ref, wq2_ref, wkv2_ref,
                 tab_ref, vones_ref,
                 h_ref, qm_ref, km_ref, vm_ref, qs_ref, ks_ref, vs_ref, qd_ref, kd_ref, vd_ref):
    x = x_ref[0]
    h = _rms(x, gpre_ref[...]) * (1.0 + scale_ref[...]) + shift_ref[...]
    hb = h.astype(BF16)
    h_ref[0] = hb
    acc = _dot(hb, w_ref[...])

    cqn = _rms(acc[:, 0:256], gq_ref[...]).astype(BF16)
    ckvn = _rms(acc[:, 256:384], gkv_ref[...]).astype(BF16)
    q2 = _dot(cqn, wq2_ref[...])
    kv2 = _dot(ckvn, wkv2_ref[...])
    tab = tab_ref[...]
    cq_t, sq_t = tab[:, 0:128], tab[:, 128:256]
    ck_t, sk_t = tab[:, 256:384], tab[:, 384:512]
    kpe = acc[:, 384:512] * ck_t + acc[:, 512:640] * sk_t
    nslot = MLA_HEADS * LANES
    for hh in range(MLA_HEADS):
        lo, hi = hh * LANES, (hh + 1) * LANES
        qm_ref[0, :, lo:hi] = (q2[:, lo:hi] * cq_t + q2[:, nslot + lo:nslot + hi] * sq_t).astype(BF16)
        km_ref[0, :, lo:hi] = (kv2[:, lo:hi] + kpe).astype(BF16)
    vm_ref[0] = (kv2[:, nslot:] + vones_ref[...]).astype(BF16)

    o = 640
    qs_ref[0] = acc[:, o:o + 512].astype(BF16)
    ks_ref[0] = acc[:, o + 512:o + 640].astype(BF16)
    vs_ref[0] = acc[:, o + 640:o + 768].astype(BF16)
    qd_ref[0] = acc[:, o + 768:o + 1280].astype(BF16)
    kd_ref[0] = acc[:, o + 1280:o + 1792].astype(BF16)
    vd_ref[0] = acc[:, o + 1792:o + 2304].astype(BF16)


def _proj(x, mod4, gpre, w_main, gq, gkv, wq2, wkv2, tab, vones):
    B, S, D = x.shape
    ts = PROJ_TS
    nw = w_main.shape[1]
    tok = lambda n: pl.BlockSpec((1, ts, n), lambda b, t: (b, t, 0))
    full = lambda a: pl.BlockSpec(a.shape, lambda b, t: (0,) * a.ndim)
    out = lambda n: jax.ShapeDtypeStruct((B, S, n), BF16)
    return pl.pallas_call(
        _proj_kernel,
        out_shape=[out(D), out(1024), out(1024), out(1024), out(512), out(128), out(128),
                   out(512), out(512), out(512)],
        grid=(B, S // ts),
        in_specs=[tok(D),
                  pl.BlockSpec((None, None, 1, D), lambda b, t: (b, 0, 0, 0)),
                  pl.BlockSpec((None, None, 1, D), lambda b, t: (b, 1, 0, 0)),
                  full(gpre),
                  pl.BlockSpec((D, nw), lambda b, t: (0, 0)),
                  full(gq), full(gkv), full(wq2), full(wkv2),
                  pl.BlockSpec((ts, 512), lambda b, t: (t, 0)),
                  full(vones)],
        out_specs=[tok(D), tok(1024), tok(1024), tok(1024), tok(512), tok(128), tok(128),
                   tok(512), tok(512), tok(512)],
        compiler_params=pltpu.CompilerParams(dimension_semantics=("arbitrary", "arbitrary"),
                                             vmem_limit_bytes=VMEM_LIMIT),
        name="proj",
    )(x, mod4, mod4, gpre, w_main, gq, gkv, wq2, wkv2, tab, vones)


def _mla_kernel(q_ref, k_ref, v_ref, o_ref):
    tq = q_ref.shape[1]
    lane = lax.broadcasted_iota(jnp.int32, (tq, LANES), 1)
    res = []
    for j in range(2):
        lo, hi = j * LANES, (j + 1) * LANES
        s = _dot_nt(q_ref[0, :, lo:hi], k_ref[0, :, lo:hi])
        m = jnp.max(s, axis=-1, keepdims=True)
        p = jnp.exp(s - m).astype(BF16)
        oa = _dot(p, v_ref[0, :, lo:hi])
        l = oa[:, 64:65] if j == 0 else oa[:, 0:1]
        res.append(oa * (1.0 / l))
    o_ref[0] = jnp.where(lane < 64, res[0], res[1])


def _mla_attn(q, k, v):
    B, S, _ = q.shape
    tq = ATT_TQ
    return pl.pallas_call(
        _mla_kernel,
        out_shape=jax.ShapeDtypeStruct((B, S, MLA_HEADS * MLA_V), F32),
        grid=(B, MLA_HEADS // 2, S // tq),
        in_specs=[pl.BlockSpec((1, tq, 2 * LANES), lambda b, p, i: (b, i, p)),
                  pl.BlockSpec((1, S, 2 * LANES), lambda b, p, i: (b, 0, p)),
                  pl.BlockSpec((1, S, 2 * LANES), lambda b, p, i: (b, 0, p))],
        out_specs=pl.BlockSpec((1, tq, LANES), lambda b, p, i: (b, i, p)),
        compiler_params=pltpu.CompilerParams(
            dimension_semantics=("arbitrary", "arbitrary", "arbitrary"),
            vmem_limit_bytes=VMEM_LIMIT),
        name="mla_attn",
    )(q, k, v)


def _swa_kernel(sink_ref, q_ref, kp_ref, kc_ref, kn_ref, vp_ref, vc_ref, vn_ref, bias_ref, o_ref,
                *, seq):
    n = pl.program_id(1)
    blk = SWA_BLOCK
    kcat = jnp.concatenate([kp_ref[0], kc_ref[0], kn_ref[0]], axis=0)
    vcat = jnp.concatenate([vp_ref[0], vc_ref[0], vn_ref[0]], axis=0)
    t = lax.broadcasted_iota(jnp.int32, (blk, 3 * blk), 0)
    j = lax.broadcasted_iota(jnp.int32, (blk, 3 * blk), 1)
    kpos = n * blk - blk + j
    valid = (j >= t) & (j <= t + 2 * SWA_WINDOW) & (kpos >= 0) & (kpos < seq)
    lane = lax.broadcasted_iota(jnp.int32, (blk, LANES), 1)
    low = lane < 64
    for pair in range(SWA_HEADS // 2):
        q2 = q_ref[0, :, pair * LANES:(pair + 1) * LANES]
        outs = []
        for half in range(2):
            hh = pair + 4 * half
            qh = jnp.where(low if half == 0 else jnp.logical_not(low), q2, jnp.zeros_like(q2))
            s = _dot_nt(qh, kcat) + bias_ref[hh]
            s = jnp.where(valid, s, NEG)
            sk = sink_ref[hh]
            m = jnp.maximum(jnp.max(s, axis=-1, keepdims=True), sk)
            p = jnp.exp(s - m)
            denom = jnp.sum(p, axis=-1, keepdims=True) + jnp.exp(sk - m)
            pn = (p * (1.0 / denom)).astype(BF16)
            outs.append(_dot(pn, vcat))
        o_ref[0, :, pair * LANES:(pair + 1) * LANES] = jnp.where(low, outs[0], outs[1])


def _swa_attn(sink, q, k, v, bias):
    B, S, _ = q.shape
    blk = SWA_BLOCK
    nb = S // blk
    kv = lambda f: pl.BlockSpec((1, blk, LANES), f)
    prev = lambda b, n: (b, jnp.maximum(n - 1, 0), 0)
    cur = lambda b, n: (b, n, 0)
    nxt = lambda b, n: (b, jnp.minimum(n + 1, nb - 1), 0)
    return pl.pallas_call(
        functools.partial(_swa_kernel, seq=S),
        out_shape=jax.ShapeDtypeStruct((B, S, SWA_HEADS * SWA_HEAD_DIM), F32),
        grid=(B, nb),
        in_specs=[pl.BlockSpec(memory_space=pltpu.SMEM),
                  pl.BlockSpec((1, blk, 512), cur),
                  kv(prev), kv(cur), kv(nxt), kv(prev), kv(cur), kv(nxt),
                  pl.BlockSpec(bias.shape, lambda b, n: (0, 0, 0))],
        out_specs=pl.BlockSpec((1, blk, 512), cur),
        compiler_params=pltpu.CompilerParams(dimension_semantics=("arbitrary", "arbitrary"),
                                             vmem_limit_bytes=VMEM_LIMIT),
        name="swa_attn",
    )(sink, q, k, k, k, v, v, v, bias)


def _diff_kernel(cb_ref, lq1_ref, lk1_ref, lq2_ref, lk2_ref, gsub_ref, q_ref, k_ref, v_ref, band_ref,
                 o_ref, s_scr, *, lam_init):
    hd = pl.program_id(1)
    i = pl.program_id(2)
    tq = q_ref.shape[1]
    nkt = s_scr.shape[1]
    kt = KEY_TILE
    lane = lax.broadcasted_iota(jnp.int32, (tq, LANES), 1)
    low = lane < 64
    q2 = q_ref[0]
    k2 = k_ref[0]
    c_left = cb_ref[hd, 0]
    c_right = cb_ref[hd, 1]

    for c in range(2):
        qc = jnp.where(low if c == 0 else jnp.logical_not(low), q2, jnp.zeros_like(q2))
        s_all = _dot_nt(qc, k2)
        for jt in range(nkt):
            cj = jnp.where(jt < i - 1, c_left, jnp.where(jt > i + 1, c_right, 0.0))
            s_scr[c, jt] = s_all[:, jt * kt:(jt + 1) * kt] + cj
        for t in range(3):
            jt = i - 1 + t

            @pl.when((jt >= 0) & (jt < nkt))
            def _():
                s_scr[c, jt] = s_scr[c, jt] + band_ref[0, :, t * kt:(t + 1) * kt]

    ps, ls = [], []
    for c in range(2):
        s = s_scr[c]
        m = jnp.max(jnp.max(s, axis=0), axis=-1, keepdims=True)
        p = jnp.exp(s - m[None])
        ls.append(jnp.sum(jnp.sum(p, axis=0), axis=-1, keepdims=True))
        ps.append(p)

    lam = (jnp.exp(jnp.sum(lq1_ref[...] * lk1_ref[...], axis=-1, keepdims=True))
           - jnp.exp(jnp.sum(lq2_ref[...] * lk2_ref[...], axis=-1, keepdims=True)) + lam_init)
    a = (ps[0] * (1.0 / ls[0])[None] - ps[1] * (lam / ls[1])[None]).astype(BF16)
    acc = jnp.zeros((tq, LANES), F32)
    for jt in range(nkt):
        acc = acc + _dot(a[jt], v_ref[0, jt * kt:(jt + 1) * kt, :])
    o_ref[0] = _rms(acc, gsub_ref[...]) * (1.0 - lam_init)


def _diff_attn(cb, lq1, lk1, lq2, lk2, gsub, q, k, v, band, lam_init):
    B, S, _ = q.shape
    tq = ATT_TQ
    nkt = S // KEY_TILE
    vec = lambda a: pl.BlockSpec(a.shape, lambda b, h, i: (0, 0))
    return pl.pallas_call(
        functools.partial(_diff_kernel, lam_init=lam_init),
        out_shape=jax.ShapeDtypeStruct((B, S, DIFF_HEADS * 2 * DIFF_HEAD_DIM), F32),
        grid=(B, DIFF_HEADS, S // tq),
        in_specs=[pl.BlockSpec(memory_space=pltpu.SMEM),
                  vec(lq1), vec(lk1), vec(lq2), vec(lk2), vec(gsub),
                  pl.BlockSpec((1, tq, LANES), lambda b, h, i: (b, i, h)),
                  pl.BlockSpec((1, S, LANES), lambda b, h, i: (b, 0, h)),
                  pl.BlockSpec((1, S, LANES), lambda b, h, i: (b, 0, h)),
                  pl.BlockSpec((1, tq, 3 * KEY_TILE), lambda b, h, i: (h, 0, 0))],
        out_specs=pl.BlockSpec((1, tq, LANES), lambda b, h, i: (b, i, h)),
        scratch_shapes=[pltpu.VMEM((2, nkt, tq, KEY_TILE), F32)],
        compiler_params=pltpu.CompilerParams(
            dimension_semantics=("arbitrary", "arbitrary", "arbitrary"),
            vmem_limit_bytes=VMEM_LIMIT),
        name="diff_attn",
    )(cb, lq1, lk1, lq2, lk2, gsub, q, k, v, band)


def _merge_kernel(x_ref, h_ref, gate_ref, oa_ref, ob_ref, oc_ref, wzg_ref, woa_ref, wob_ref, woc_ref,
                  wout_ref, gpost_ref, out_ref):
    hb = h_ref[0]
    d = D_MODEL
    mix = None
    for n, (o_ref, wo_ref) in enumerate(((oa_ref, woa_ref), (ob_ref, wob_ref), (oc_ref, woc_ref))):
        z = _dot(hb, wzg_ref[:, n * 512:(n + 1) * 512])
        u = (o_ref[0] * (z * _sigmoid(z))).astype(BF16)
        y = _dot(u, wo_ref[...])
        g = _sigmoid(_dot(hb, wzg_ref[:, 1536 + n * d:1536 + (n + 1) * d]))
        mix = g * y if mix is None else mix + g * y
    y = _dot(mix.astype(BF16), wout_ref[...])
    out_ref[0] = x_ref[0] + gate_ref[...] * _rms(y, gpost_ref[...])


def _merge(x, h, mod4, oa, ob, oc, wzg, woa, wob, woc, wout, gpost):
    B, S, D = x.shape
    ts = PROJ_TS
    tok = lambda n: pl.BlockSpec((1, ts, n), lambda b, t: (b, t, 0))
    full = lambda a: pl.BlockSpec(a.shape, lambda b, t: (0,) * a.ndim)
    return pl.pallas_call(
        _merge_kernel,
        out_shape=jax.ShapeDtypeStruct((B, S, D), F32),
        grid=(B, S // ts),
        in_specs=[tok(D), tok(D),
                  pl.BlockSpec((None, None, 1, D), lambda b, t: (b, 2, 0, 0)),
                  tok(512), tok(512), tok(512),
                  full(wzg), full(woa), full(wob), full(woc), full(wout), full(gpost)],
        out_specs=tok(D),
        compiler_params=pltpu.CompilerParams(dimension_semantics=("arbitrary", "arbitrary"),
                                             vmem_limit_bytes=VMEM_LIMIT),
        name="merge",
    )(x, h, mod4, oa, ob, oc, wzg, woa, wob, woc, wout, gpost)


def _swa_cols():
    return jnp.concatenate([jnp.arange(h * 64, (h + 1) * 64) for h in _SWA_HEAD_ORDER])


def _layer_weights(w_in, w_uq, w_ukv):
    D = w_in.shape[0]
    col = lambda name: w_in[:, _OFF[name][0]:_OFF[name][1]]
    z = lambda n: jnp.zeros((D, n), w_in.dtype)
    kr = col("kr")
    kr_sw = jnp.concatenate([-kr[:, 16:32], kr[:, 0:16]], axis=1)
    swa = _swa_cols()
    w_main = jnp.concatenate([
        col("cq"), col("ckv"),
        z(64), kr, z(32),
        z(64), kr_sw, z(32),
        col("qb")[:, swa] * 0.125, col("kb"), col("vb"),
        col("qc") * 0.125, col("kc"), col("vc")], axis=1).astype(BF16)
    wzg = jnp.concatenate([col("za"), col("zb")[:, swa], col("zc"), col("g")], axis=1).astype(BF16)

    r = w_uq.shape[0]
    uq = w_uq.reshape(r, MLA_HEADS, MLA_NOPE + MLA_ROPE)
    zq = lambda n: jnp.zeros((r, MLA_HEADS, n), w_uq.dtype)
    rope1, rope2 = uq[:, :, 64:80], uq[:, :, 80:96]
    q_plain = jnp.concatenate([uq, zq(32)], axis=2).reshape(r, MLA_HEADS * LANES)
    q_swap = jnp.concatenate([zq(64), -rope2, rope1, zq(32)], axis=2).reshape(r, MLA_HEADS * LANES)
    wq2 = jnp.concatenate([q_plain, q_swap], axis=1).astype(BF16)

    rk = w_ukv.shape[0]
    ukv = w_ukv.reshape(rk, MLA_HEADS, MLA_NOPE + MLA_V)
    zk = jnp.zeros((rk, MLA_HEADS, 64), w_ukv.dtype)
    k_slots = jnp.concatenate([ukv[:, :, :64], zk], axis=2).reshape(rk, MLA_HEADS * LANES)
    uv = ukv[:, :, 64:].reshape(rk, MLA_HEADS // 2, 2, 64)
    zv = jnp.zeros((rk, MLA_HEADS // 2, 64), w_ukv.dtype)
    v_slots = jnp.concatenate([uv[:, :, 0], zv, zv, uv[:, :, 1]], axis=2).reshape(rk, MLA_HEADS * LANES)
    wkv2 = jnp.concatenate([k_slots, v_slots], axis=1).astype(BF16)
    return w_main, wzg, wq2, wkv2


def kernel(x, c, w_ada, b_ada, g_pre, g_post, w_in, g_q, w_uq, g_kv, w_ukv, sink, lam_q1, lam_k1,
           lam_q2, lam_k2, g_sub, w_o_mla, w_o_swa, w_o_diff, w_out, rel_bias):
    B, S, D = x.shape
    L = w_in.shape[0]

    pos = jnp.arange(S, dtype=F32)
    inv = ROPE_THETA ** (-jnp.arange(0, MLA_ROPE, 2, dtype=F32) / MLA_ROPE)
    ang = pos[:, None] * inv[None, :]
    cos2 = jnp.tile(jnp.cos(ang), (1, 2))
    sin2 = jnp.tile(jnp.sin(ang), (1, 2))
    qscale = (MLA_NOPE + MLA_ROPE) ** -0.5
    z64, z32, one64 = jnp.zeros((S, 64), F32), jnp.zeros((S, 32), F32), jnp.ones((S, 64), F32)
    tab = jnp.concatenate([one64 * qscale, cos2 * qscale, z32,
                           z64, sin2 * qscale, z32,
                           z64, cos2, z32,
                           z64, sin2, z32], axis=1)
    slot = jnp.arange(MLA_HEADS * LANES)
    vones = jnp.where((slot // LANES) % 2 == 0, slot % LANES == 64, slot % LANES == 0)
    vones = vones.astype(F32)[None, :]

    rel_band = jnp.arange(3 * SWA_BLOCK)[None, :] - SWA_BLOCK - jnp.arange(SWA_BLOCK)[:, None]
    swa_bias = rel_bias[_t5_bucket(rel_band)][..., :SWA_HEADS].astype(F32).transpose(2, 0, 1)
    rel_d = jnp.arange(3 * KEY_TILE)[None, :] - KEY_TILE - jnp.arange(ATT_TQ)[:, None]
    diff_band = rel_bias[_t5_bucket(rel_d)][..., SWA_HEADS:].astype(F32).transpose(2, 0, 1)
    far = jnp.array([-REL_MAX_DIST, REL_MAX_DIST], jnp.int32)
    diff_far = rel_bias[_t5_bucket(far)][:, SWA_HEADS:].astype(F32).T

    mod = _ada(c, w_ada, b_ada)
    swa = _swa_cols()
    row = lambda v: v[None, :]
    for l in range(L):
        w_main, wzg, wq2, wkv2 = _layer_weights(w_in[l], w_uq[l], w_ukv[l])
        mod4 = mod[l].reshape(B, 3, 1, D)
        (h, qm, km, vm, qs, ks, vs, qd, kd, vd) = _proj(
            x, mod4, row(g_pre[l]), w_main, row(g_q[l]), row(g_kv[l]), wq2, wkv2, tab, vones)
        o_a = _mla_attn(qm, km, vm)
        o_b = _swa_attn(sink[l], qs, ks, vs, swa_bias)
        lam_init = 0.8 - 0.6 * math.exp(-0.3 * l)
        o_c = _diff_attn(diff_far, row(lam_q1[l]), row(lam_k1[l]), row(lam_q2[l]), row(lam_k2[l]),
                         row(g_sub[l]), qd, kd, vd, diff_band, lam_init)
        x = _merge(x, h, mod4, o_a, o_b, o_c, wzg,
                   w_o_mla[l].astype(BF16), w_o_swa[l][swa, :].astype(BF16),
                   w_o_diff[l].astype(BF16), w_out[l].astype(BF16), row(g_post[l]))
    return x
```
